```python
import jax, jax.numpy as jnp
from jax import lax
import numpy as np

D_MODEL = 2048
BATCH = 4
SEQ = 8192
DEPTH = 1

CHUNK = 64
N_MEM = 256
EPS = 1e-6

GMLP_BLOCK = 128
GMLP_WIDTH = D_MODEL // 2
GMLP_GROUPS = 8
GMLP_GROUP_DIM = GMLP_WIDTH // GMLP_GROUPS
CONV_WIDTH = D_MODEL // 2
CONV_K = 3
XATTN_HEADS = 4
XATTN_HEAD_DIM = 256
XATTN_WIDTH = XATTN_HEADS * XATTN_HEAD_DIM
N_BRANCH = 3
BRANCH_WIDTH = D_MODEL // 2
COL_A = 2 * GMLP_WIDTH
COL_B = 3 * CONV_WIDTH
COL_C = XATTN_WIDTH
COL_G = N_BRANCH * D_MODEL
D_IN = COL_A + COL_B + COL_C + COL_G
PEER_HEADS = 8
PEER_KEYS = 128
PEER_EXPERTS = PEER_KEYS * PEER_KEYS
PEER_TOPK = 16
PEER_HALF = 128
PEER_TOKEN_BLOCK = 128

kernel_name = 'hybrid_streaming_gmlp_conv_xattn_peer'


def rmsnorm(x, g):
    xf = x.astype(jnp.float32)
    y = xf * lax.rsqrt(jnp.mean(jnp.square(xf), axis=-1, keepdims=True) + EPS)
    return (y * g).astype(x.dtype)


def layernorm(x, g, b):
    xf = x.astype(jnp.float32)
    mu = jnp.mean(xf, axis=-1, keepdims=True)
    var = jnp.mean(jnp.square(xf - mu), axis=-1, keepdims=True)
    return ((xf - mu) * lax.rsqrt(var + EPS) * g + b).astype(x.dtype)


def chunk_causal_block_mask():
    c = jnp.arange(GMLP_BLOCK) // CHUNK
    return c[None, :] <= c[:, None]


def spatial_gating(z, w_s, b_s, ln_g, ln_b):
    u, v = jnp.split(z, 2, axis=-1)
    v = layernorm(v, ln_g, ln_b)
    bsz, s, _ = v.shape
    nb = s // GMLP_BLOCK
    v = v.reshape(bsz, nb, GMLP_BLOCK, GMLP_GROUPS, GMLP_GROUP_DIM)
    w = jnp.where(chunk_causal_block_mask()[None], w_s, 0.0).astype(v.dtype)
    sv = jnp.einsum('gij,bnjgc->bnigc', w, v) + b_s.T[:, :, None]
    return u * sv.reshape(bsz, s, GMLP_WIDTH)


def short_conv_mixer(z, conv_w):
    b_gate, c_gate, h = jnp.split(z, 3, axis=-1)
    y = lax.conv_general_dilated(
        c_gate * h, conv_w[:, None, :].astype(h.dtype),
        window_strides=(1,), padding=[(CONV_K - 1, 0)],
        dimension_numbers=('NWC', 'WIO', 'NWC'),
        feature_group_count=CONV_WIDTH)
    return b_gate * y


def memory_cross_attention(q, mem_n, w_kv):
    bsz, s, _ = q.shape
    m = mem_n.shape[1]
    k, v = jnp.split(mem_n @ w_kv, 2, axis=-1)
    q = q.reshape(bsz, s, XATTN_HEADS, XATTN_HEAD_DIM)
    k = k.reshape(bsz, m, XATTN_HEADS, XATTN_HEAD_DIM)
    v = v.reshape(bsz, m, XATTN_HEADS, XATTN_HEAD_DIM)
    sc = jnp.einsum('bshd,bmhd->bhsm', q, k).astype(jnp.float32) * (XATTN_HEAD_DIM ** -0.5)
    p = jax.nn.softmax(sc, axis=-1).astype(v.dtype)
    o = jnp.einsum('bhsm,bmhd->bshd', p, v)
    return o.reshape(bsz, s, XATTN_WIDTH)


def hybrid_mixer(h, mem_n, w_in, w_s, b_s, ln_g, ln_b, conv_w, w_kv, w_branch, w_out):
    bsz, s, _ = h.shape
    z = h @ w_in
    z_a, z_b, z_c, z_g = jnp.split(z, [COL_A, COL_A + COL_B, COL_A + COL_B + COL_C], axis=-1)
    gates = z_g.reshape(bsz, s, N_BRANCH, D_MODEL)
    branches = (spatial_gating(jax.nn.gelu(z_a), w_s, b_s, ln_g, ln_b),
                short_conv_mixer(z_b, conv_w),
                memory_cross_attention(z_c, mem_n, w_kv))
    merged = jnp.zeros((bsz, s, D_MODEL), h.dtype)
    for n in range(N_BRANCH):
        merged = merged + jax.nn.sigmoid(gates[:, :, n]) * (branches[n] @ w_branch[n])
    return merged @ w_out


def peer_retrieve(h, w_q, sub_keys):
    t = h.shape[0]
    q = (h @ w_q).reshape(t, PEER_HEADS, 2, PEER_HALF)
    sc = jnp.einsum('thpc,hpkc->thpk', q, sub_keys).astype(jnp.float32)
    top_s, top_i = lax.top_k(sc, PEER_TOPK)
    cand_s = (top_s[:, :, 0, :, None] + top_s[:, :, 1, None, :]).reshape(t, PEER_HEADS, PEER_TOPK * PEER_TOPK)
    cand_i = (top_i[:, :, 0, :, None] * PEER_KEYS + top_i[:, :, 1, None, :]).reshape(t, PEER_HEADS, PEER_TOPK * PEER_TOPK)
    best_s, pos = lax.top_k(cand_s, PEER_TOPK)
    idx = jnp.take_along_axis(cand_i, pos, axis=-1)
    g = jax.nn.softmax(best_s, axis=-1)
    return idx, g


def peer_experts(h, idx, g, w_down, w_up):
    t, d = h.shape
    nb = t // PEER_TOKEN_BLOCK

    def block(args):
        hb, ib, gb = args
        u = jnp.take(w_down, ib, axis=0)
        a = jax.nn.gelu(jnp.einsum('thkd,td->thk', u, hb))
        v = jnp.take(w_up, ib, axis=0)
        return jnp.einsum('thk,thkd->td', a * gb.astype(a.dtype), v)

    out = lax.map(block, (h.reshape(nb, PEER_TOKEN_BLOCK, d),
                          idx.reshape(nb, PEER_TOKEN_BLOCK, PEER_HEADS, PEER_TOPK),
                          g.reshape(nb, PEER_TOKEN_BLOCK, PEER_HEADS, PEER_TOPK)))
    return out.reshape(t, d)


def setup_inputs(seed: int = 0) -> dict:
    key = jax.random.key(seed)
    ks = jax.random.split(key, 20)
    f32 = jnp.float32
    nrm = lambda k, shape, scale: jax.random.normal(k, shape, f32) * scale
    gain = lambda k, shape: 1.0 + 0.02 * jax.random.normal(k, shape, f32)
    L = DEPTH
    return {
        'x': jax.random.normal(ks[0], (BATCH, SEQ, D_MODEL), f32),
        'mem': jax.random.normal(ks[1], (BATCH, N_MEM, D_MODEL), f32),
        'norm_mix_g': gain(ks[2], (L, D_MODEL)),
        'norm_mem_g': gain(ks[3], (L, D_MODEL)),
        'w_in': nrm(ks[4], (L, D_MODEL, D_IN), D_MODEL ** -0.5),
        'gmlp_w_s': nrm(ks[5], (L, GMLP_GROUPS, GMLP_BLOCK, GMLP_BLOCK), 0.5 * GMLP_BLOCK ** -0.5),
        'gmlp_b_s': 1.0 + 0.01 * jax.random.normal(ks[6], (L, GMLP_GROUPS, GMLP_BLOCK), f32),
        'gmlp_ln_g': gain(ks[7], (L, GMLP_WIDTH)),
        'gmlp_ln_b': nrm(ks[8], (L, GMLP_WIDTH), 0.01),
        'conv_w': nrm(ks[9], (L, CONV_K, CONV_WIDTH), CONV_K ** -0.5),
        'w_kv': nrm(ks[10], (L, D_MODEL, 2 * XATTN_WIDTH), D_MODEL ** -0.5),
        'w_branch': nrm(ks[11], (L, N_BRANCH, BRANCH_WIDTH, D_MODEL), BRANCH_WIDTH ** -0.5),
        'w_out': nrm(ks[12], (L, D_MODEL, D_MODEL), D_MODEL ** -0.5),
        'norm_ffn_g': gain(ks[13], (L, D_MODEL)),
        'peer_w_q': nrm(ks[14], (L, D_MODEL, PEER_HEADS * 2 * PEER_HALF), D_MODEL ** -0.5),
        'peer_keys': nrm(ks[15], (L, PEER_HEADS, 2, PEER_KEYS, PEER_HALF), PEER_HALF ** -0.5),
        'peer_w_down': nrm(ks[16], (L, PEER_EXPERTS, D_MODEL), D_MODEL ** -0.5),
        'peer_w_up': nrm(ks[17], (L, PEER_EXPERTS, D_MODEL), PEER_HEADS ** -0.5),
        'norm_final_g': gain(ks[18], (D_MODEL,)),
    }


def reference(x, mem, norm_mix_g, norm_mem_g, w_in, gmlp_w_s, gmlp_b_s, gmlp_ln_g, gmlp_ln_b,
              conv_w, w_kv, w_branch, w_out, norm_ffn_g, peer_w_q, peer_keys, peer_w_down,
              peer_w_up, norm_final_g):
    bsz, s, d = x.shape
    for l in range(DEPTH):
        h = rmsnorm(x, norm_mix_g[l])
        m = rmsnorm(mem, norm_mem_g[l])
        x = x + hybrid_mixer(h, m, w_in[l], gmlp_w_s[l], gmlp_b_s[l], gmlp_ln_g[l], gmlp_ln_b[l],
                             conv_w[l], w_kv[l], w_branch[l], w_out[l])
        h = rmsnorm(x, norm_ffn_g[l]).reshape(bsz * s, d)
        idx, g = peer_retrieve(h, peer_w_q[l], peer_keys[l])
        x = x + peer_experts(h, idx, g, peer_w_down[l], peer_w_up[l]).reshape(bsz, s, d)
    return rmsnorm(x, norm_final_g)
```

```python
import functools

import jax
import jax.numpy as jnp
from jax import lax
from jax.experimental import pallas as pl
from jax.experimental.pallas import tpu as pltpu

F32 = jnp.float32
BF16 = jnp.bfloat16
EPS = 1e-6

V7X_LANES = 128
V7X_BF16_SUBLANES = 16
V7X_VMEM_BYTES = 64 * 1024 * 1024
VMEM_LIMIT = V7X_VMEM_BYTES - 8 * 1024 * 1024

CHUNK = 64
GMLP_BLOCK = 128
GMLP_GROUPS = 8
CONV_K = 3
XATTN_HEADS = 4
N_BRANCH = 3
PEER_HEADS = 8
PEER_KEYS = 128
PEER_TOPK = 16
PEER_HALF = 128


def _params(sem):
    return pltpu.CompilerParams(dimension_semantics=sem, vmem_limit_bytes=VMEM_LIMIT)


def _rms(x, g):
    ms = jnp.mean(x * x, axis=-1, keepdims=True)
    return x * lax.rsqrt(ms + EPS) * g


def _kv_kernel(mem_ref, g_ref, w_ref, k_ref, v_ref):
    mn = _rms(mem_ref[...], g_ref[...]).astype(BF16)
    kv = jnp.dot(mn, w_ref[...], preferred_element_type=F32)
    xw = k_ref.shape[-1]
    k_ref[...] = kv[:, :xw].astype(BF16)
    v_ref[...] = kv[:, xw:].astype(BF16)


def _kv(mem, g, w_kv):
    b, m, d = mem.shape
    xw = w_kv.shape[1] // 2
    return pl.pallas_call(
        _kv_kernel,
        grid=(b,),
        in_specs=[
            pl.BlockSpec((None, m, d), lambda i: (i, 0, 0)),
            pl.BlockSpec((1, d), lambda i: (0, 0)),
            pl.BlockSpec((d, 2 * xw), lambda i: (0, 0)),
        ],
        out_specs=[
            pl.BlockSpec((None, m, xw), lambda i: (i, 0, 0)),
            pl.BlockSpec((None, m, xw), lambda i: (i, 0, 0)),
        ],
        out_shape=[jax.ShapeDtypeStruct((b, m, xw), BF16)] * 2,
        compiler_params=_params(("parallel",)),
        name="kv",
    )(mem, g, w_kv)


def _inproj_kernel(x_ref, g_ref, w_ref, z_ref, h_scr, *, gelu_blocks, gate_block0):
    j = pl.program_id(1)

    @pl.when(j == 0)
    def _():
        h_scr[...] = _rms(x_ref[...], g_ref[...]).astype(BF16)

    acc = jnp.dot(h_scr[...], w_ref[...], preferred_element_type=F32)

    @pl.when(j < gelu_blocks)
    def _():
        z_ref[...] = jax.nn.gelu(acc).astype(BF16)

    @pl.when(jnp.logical_and(j >= gelu_blocks, j < gate_block0))
    def _():
        z_ref[...] = acc.astype(BF16)

    @pl.when(j >= gate_block0)
    def _():
        z_ref[...] = jax.nn.sigmoid(acc).astype(BF16)


def _inproj(x2, g, w_in, *, tm, tn, col_gelu, col_gate):
    t, d = x2.shape
    n = w_in.shape[1]
    kern = functools.partial(_inproj_kernel, gelu_blocks=col_gelu // tn, gate_block0=col_gate // tn)
    return pl.pallas_call(
        kern,
        grid=(t // tm, n // tn),
        in_specs=[
            pl.BlockSpec((tm, d), lambda i, j: (i, 0)),
            pl.BlockSpec((1, d), lambda i, j: (0, 0)),
            pl.BlockSpec((d, tn), lambda i, j: (0, j)),
        ],
        out_specs=pl.BlockSpec((tm, tn), lambda i, j: (i, j)),
        out_shape=jax.ShapeDtypeStruct((t, n), BF16),
        scratch_shapes=[pltpu.VMEM((tm, d), BF16)],
        compiler_params=_params(("parallel", "arbitrary")),
        name="in_proj",
    )(x2, g, w_in)


def _branch_kernel(u_ref, v_ref, bg_ref, cg_ref, hs_ref, cgp_ref, hsp_ref, q_ref, k_ref, vm_ref,
                   ws_ref, bias_ref, lng_ref, lnb_ref, cw_ref,
                   ya_ref, yb_ref, yc_ref, pext_ref, *, tm, seq, halo, head_dim):
    i = pl.program_id(0)
    gd = GMLP_BLOCK

    vf = v_ref[...].astype(F32)
    mu = jnp.mean(vf, axis=-1, keepdims=True)
    dv = vf - mu
    var = jnp.mean(dv * dv, axis=-1, keepdims=True)
    vn = (dv * lax.rsqrt(var + EPS) * lng_ref[...] + lnb_ref[...]).astype(BF16)
    row_chunk = lax.broadcasted_iota(jnp.int32, (gd, gd), 0) // CHUNK
    col_chunk = lax.broadcasted_iota(jnp.int32, (gd, gd), 1) // CHUNK
    readable = col_chunk <= row_chunk
    for g in range(GMLP_GROUPS):
        wm = jnp.where(readable, ws_ref[g], 0.0).astype(BF16)
        cs = slice(g * gd, (g + 1) * gd)
        for n in range(tm // gd):
            rs = slice(n * gd, (n + 1) * gd)
            sv = jnp.dot(wm, vn[rs, cs], preferred_element_type=F32) + bias_ref[:, cs]
            ya_ref[rs, cs] = (u_ref[rs, cs].astype(F32) * sv).astype(BF16)

    p = cg_ref[...].astype(F32) * hs_ref[...].astype(F32)
    pp = cgp_ref[...].astype(F32) * hsp_ref[...].astype(F32)
    first_of_sequence = (i * tm) % seq == 0
    pp = jnp.where(first_of_sequence, 0.0, pp)
    pext_ref[0:halo, :] = pp
    pext_ref[halo:halo + tm, :] = p
    p1 = pext_ref[halo - 1:halo - 1 + tm, :]
    p2 = pext_ref[halo - 2:halo - 2 + tm, :]
    conv = cw_ref[2:3, :] * p + cw_ref[1:2, :] * p1 + cw_ref[0:1, :] * p2
    yb_ref[...] = (bg_ref[...].astype(F32) * conv).astype(BF16)

    scale = head_dim ** -0.5
    for h in range(XATTN_HEADS):
        hs = slice(h * head_dim, (h + 1) * head_dim)
        s = lax.dot_general(q_ref[:, hs], k_ref[:, hs], (((1,), (1,)), ((), ())),
                            preferred_element_type=F32) * scale
        e = jnp.exp(s - jnp.max(s, axis=-1, keepdims=True))
        l = jnp.sum(e, axis=-1, keepdims=True)
        pr = (e * (1.0 / l)).astype(BF16)
        yc_ref[:, hs] = jnp.dot(pr, vm_ref[:, hs], preferred_element_type=F32).astype(BF16)


def _branches(z, k, vm, w_s, bias_full, ln_g, ln_b, conv_w, *, tm, seq, w):
    t = z.shape[0]
    m = k.shape[1]
    halo = V7X_BF16_SUBLANES
    hb = tm // halo
    tiles_per_seq = seq // tm
    head_dim = w // XATTN_HEADS

    def col(c):
        return pl.BlockSpec((tm, w), lambda i, c=c: (i, c))

    def prev(c):
        return pl.BlockSpec((halo, w), lambda i, c=c: (jnp.maximum(i * hb - 1, 0), c))

    kern = functools.partial(_branch_kernel, tm=tm, seq=seq, halo=halo, head_dim=head_dim)
    const2 = lambda i: (0, 0)
    return pl.pallas_call(
        kern,
        grid=(t // tm,),
        in_specs=[
            col(0), col(1), col(2), col(3), col(4), prev(3), prev(4), col(5),
            pl.BlockSpec((None, m, w), lambda i: (i // tiles_per_seq, 0, 0)),
            pl.BlockSpec((None, m, w), lambda i: (i // tiles_per_seq, 0, 0)),
            pl.BlockSpec(w_s.shape, lambda i: (0, 0, 0)),
            pl.BlockSpec(bias_full.shape, const2),
            pl.BlockSpec((1, w), const2),
            pl.BlockSpec((1, w), const2),
            pl.BlockSpec(conv_w.shape, const2),
        ],
        out_specs=[pl.BlockSpec((tm, w), lambda i: (i, 0))] * 3,
        out_shape=[jax.ShapeDtypeStruct((t, w), BF16)] * 3,
        scratch_shapes=[pltpu.VMEM((halo + tm, w), F32)],
        compiler_params=_params(("parallel",)),
        name="branches",
    )(z, z, z, z, z, z, z, z, k, vm, w_s, bias_full, ln_g, ln_b, conv_w)


def _merge_kernel(ya_ref, yb_ref, yc_ref, ga_ref, gb_ref, gc_ref, w_ref, o_ref):
    acc = None
    for n, (y_ref, g_ref) in enumerate(((ya_ref, ga_ref), (yb_ref, gb_ref), (yc_ref, gc_ref))):
        t = g_ref[...].astype(F32) * jnp.dot(y_ref[...], w_ref[n], preferred_element_type=F32)
        acc = t if acc is None else acc + t
    o_ref[...] = acc.astype(BF16)


def _merge(ya, yb, yc, z, w_branch, *, tm, tn, col_gate):
    t, w = ya.shape
    d = w_branch.shape[2]
    gate_blocks = d // tn
    gb0 = col_gate // tn

    def gate(n):
        return pl.BlockSpec((tm, tn), lambda i, j, n=n: (i, gb0 + n * gate_blocks + j))

    y_spec = pl.BlockSpec((tm, w), lambda i, j: (i, 0))
    return pl.pallas_call(
        _merge_kernel,
        grid=(t // tm, d // tn),
        in_specs=[y_spec, y_spec, y_spec, gate(0), gate(1), gate(2),
                  pl.BlockSpec((N_BRANCH, w, tn), lambda i, j: (0, 0, j))],
        out_specs=pl.BlockSpec((tm, tn), lambda i, j: (i, j)),
        out_shape=jax.ShapeDtypeStruct((t, d), BF16),
        compiler_params=_params(("parallel", "arbitrary")),
        name="merge",
    )(ya, yb, yc, z, z, z, w_branch)


def _outproj_kernel(m_ref, x_ref, w_ref, g_ref, x1_ref, ht_ref):
    x1 = x_ref[...] + jnp.dot(m_ref[...], w_ref[...], preferred_element_type=F32)
    x1_ref[...] = x1
    ht_ref[...] = _rms(x1, g_ref[...]).T.astype(BF16)


def _outproj(merged, x2, w_out, g, *, tm):
    t, d = x2.shape
    return pl.pallas_call(
        _outproj_kernel,
        grid=(t // tm,),
        in_specs=[
            pl.BlockSpec((tm, d), lambda i: (i, 0)),
            pl.BlockSpec((tm, d), lambda i: (i, 0)),
            pl.BlockSpec((d, d), lambda i: (0, 0)),
            pl.BlockSpec((1, d), lambda i: (0, 0)),
        ],
        out_specs=[pl.BlockSpec((tm, d), lambda i: (i, 0)), pl.BlockSpec((d, tm), lambda i: (0, i))],
        out_shape=[jax.ShapeDtypeStruct((t, d), F32), jax.ShapeDtypeStruct((d, t), BF16)],
        compiler_params=_params(("parallel",)),
        name="out_proj",
    )(merged, x2, w_out, g)


def _top_values(s, top_ref, k):
    cur = s
    for r in range(k):
        m = jnp.max(cur, axis=0, keepdims=True)
        top_ref[r:r + 1, :] = m
        if r + 1 < k:
            cur = jnp.where(cur == m, -jnp.inf, cur)


def _retrieve_kernel(ht_ref, wq_ref, keys_ref, s1_ref, s2_ref, w1_ref, e2_ref, tau_ref,
                     q_scr, ta_scr, tb_scr, tc_scr):
    kk = PEER_TOPK
    q_scr[...] = jnp.dot(wq_ref[...], ht_ref[...], preferred_element_type=F32).astype(BF16)
    for h in range(PEER_HEADS):
        r0 = h * 2 * PEER_HALF
        s1 = jnp.dot(keys_ref[h, 0], q_scr[r0:r0 + PEER_HALF, :], preferred_element_type=F32)
        s2 = jnp.dot(keys_ref[h, 1], q_scr[r0 + PEER_HALF:r0 + 2 * PEER_HALF, :],
                     preferred_element_type=F32)
        _top_values(s1, ta_scr, kk)
        _top_values(s2, tb_scr, kk)
        half = kk // 2
        cands = [ta_scr[0:1, :] + tb_scr[0:kk, :]]
        for i in range(1, half):
            cands.append(ta_scr[i:i + 1, :] + tb_scr[0:half, :])
        cands.append(ta_scr[half:kk, :] + tb_scr[0:1, :])
        _top_values(jnp.concatenate(cands, axis=0), tc_scr, kk)
        top = tc_scr[0:kk, :]
        best = tc_scr[0:1, :]
        zsum = jnp.sum(jnp.exp(top - best), axis=0, keepdims=True)
        s1_ref[h] = s1
        s2_ref[h] = s2
        w1_ref[h] = jnp.exp(s1 - ta_scr[0:1, :]) * (1.0 / zsum)
        e2_ref[h] = jnp.exp(s2 - tb_scr[0:1, :])
        tau_ref[h] = tc_scr[kk - 1:kk, :]


def _retrieve(ht, wq_t, keys, *, tm):
    d, t = ht.shape
    nq = wq_t.shape[0]
    hk = (PEER_HEADS, PEER_KEYS, t)
    blk = pl.BlockSpec((PEER_HEADS, PEER_KEYS, tm), lambda i: (0, 0, i))
    return pl.pallas_call(
        _retrieve_kernel,
        grid=(t // tm,),
        in_specs=[
            pl.BlockSpec((d, tm), lambda i: (0, i)),
            pl.BlockSpec((nq, d), lambda i: (0, 0)),
            pl.BlockSpec(keys.shape, lambda i: (0, 0, 0, 0)),
        ],
        out_specs=[blk, blk, blk, blk, pl.BlockSpec((PEER_HEADS, 1, tm), lambda i: (0, 0, i))],
        out_shape=[jax.ShapeDtypeStruct(hk, F32)] * 4 + [jax.ShapeDtypeStruct((PEER_HEADS, 1, t), F32)],
        scratch_shapes=[pltpu.VMEM((nq, tm), BF16)] + [pltpu.VMEM((PEER_TOPK, tm), F32)] * 3,
        compiler_params=_params(("parallel",)),
        name="retrieve",
    )(ht, wq_t, keys)


def _experts_kernel(ht_ref, wd_ref, wut_ref, s1_ref, w1_ref, s2_ref, e2_ref, tau_ref, o_ref,
                    p_scr, *, rows, tm):
    j = pl.program_id(1)
    nk = PEER_KEYS
    a = jax.nn.gelu(jnp.dot(wd_ref[...], ht_ref[...], preferred_element_type=F32))
    for r in range(rows):
        for c in range(tm // V7X_LANES):
            cs = slice(c * V7X_LANES, (c + 1) * V7X_LANES)
            gate = None
            for h in range(PEER_HEADS):
                ssum = s1_ref[h, r:r + 1, cs] + s2_ref[h, :, cs]
                val = e2_ref[h, :, cs] * w1_ref[h, r:r + 1, cs]
                term = jnp.where(ssum >= tau_ref[h, :, cs], val, 0.0)
                gate = term if gate is None else gate + term
            p_scr[r * nk:(r + 1) * nk, cs] = (a[r * nk:(r + 1) * nk, cs] * gate).astype(BF16)
    contrib = jnp.dot(wut_ref[...], p_scr[...], preferred_element_type=F32)

    @pl.when(j == 0)
    def _():
        o_ref[...] = contrib

    @pl.when(j > 0)
    def _():
        o_ref[...] += contrib


def _experts(ht, wd, wu_t, s1, w1, s2, e2, tau, *, tm, rows):
    d, t = ht.shape
    ne = wd.shape[0]
    eb = rows * PEER_KEYS
    kern = functools.partial(_experts_kernel, rows=rows, tm=tm)
    row_blk = pl.BlockSpec((PEER_HEADS, rows, tm), lambda i, j: (0, j, i))
    full_blk = pl.BlockSpec((PEER_HEADS, PEER_KEYS, tm), lambda i, j: (0, 0, i))
    return pl.pallas_call(
        kern,
        grid=(t // tm, ne // eb),
        in_specs=[
            pl.BlockSpec((d, tm), lambda i, j: (0, i)),
            pl.BlockSpec((eb, d), lambda i, j: (j, 0)),
            pl.BlockSpec((d, eb), lambda i, j: (0, j)),
            row_blk, row_blk, full_blk, full_blk,
            pl.BlockSpec((PEER_HEADS, 1, tm), lambda i, j: (0, 0, i)),
        ],
        out_specs=pl.BlockSpec((d, tm), lambda i, j: (0, i)),
        out_shape=jax.ShapeDtypeStruct((d, t), F32),
        scratch_shapes=[pltpu.VMEM((eb, tm), BF16)],
        compiler_params=_params(("parallel", "arbitrary")),
        name="experts",
    )(ht, wd, wu_t, s1, w1, s2, e2, tau)


def _final_kernel(x1_ref, pt_ref, g_ref, o_ref, *, normalize):
    y = x1_ref[...] + pt_ref[...].T
    o_ref[...] = _rms(y, g_ref[...]) if normalize else y


def _final(x1, peer_t, g, *, tm, normalize):
    t, d = x1.shape
    return pl.pallas_call(
        functools.partial(_final_kernel, normalize=normalize),
        grid=(t // tm,),
        in_specs=[
            pl.BlockSpec((tm, d), lambda i: (i, 0)),
            pl.BlockSpec((d, tm), lambda i: (0, i)),
            pl.BlockSpec((1, d), lambda i: (0, 0)),
        ],
        out_specs=pl.BlockSpec((tm, d), lambda i: (i, 0)),
        out_shape=jax.ShapeDtypeStruct((t, d), F32),
        compiler_params=_params(("parallel",)),
        name="final",
    )(x1, peer_t, g)


def _tiles(t, seq):
    return dict(
        inproj_tm=min(512, seq), inproj_tn=1024,
        branch_tm=min(256, seq),
        merge_tm=min(512, seq), merge_tn=1024,
        outproj_tm=min(512, seq),
        retrieve_tm=min(512, t),
        experts_tm=min(512, t), experts_rows=8,
        final_tm=min(512, t),
    )


def _layer(x2, mem, seq, norm_mix_g, norm_mem_g, w_in, w_s, b_s, ln_g, ln_b, conv_w, w_kv,
           w_branch, w_out, norm_ffn_g, w_q, keys, w_down, w_up):
    t, d = x2.shape
    w = ln_g.shape[0]
    tl = _tiles(t, seq)
    col_gelu = 2 * w
    col_gate = w_in.shape[1] - N_BRANCH * d
    row = lambda a: a.reshape(1, -1)

    k, vm = _kv(mem, row(norm_mem_g), w_kv.astype(BF16))
    z = _inproj(x2, row(norm_mix_g), w_in.astype(BF16), tm=tl["inproj_tm"], tn=tl["inproj_tn"],
                col_gelu=col_gelu, col_gate=col_gate)
    bias_full = jnp.repeat(b_s.T, w // GMLP_GROUPS, axis=1)
    ya, yb, yc = _branches(z, k, vm, w_s, bias_full, row(ln_g), row(ln_b), conv_w,
                           tm=tl["branch_tm"], seq=seq, w=w)
    merged = _merge(ya, yb, yc, z, w_branch.astype(BF16), tm=tl["merge_tm"], tn=tl["merge_tn"],
                    col_gate=col_gate)
    x1, ht = _outproj(merged, x2, w_out.astype(BF16), row(norm_ffn_g), tm=tl["outproj_tm"])
    s1, s2, w1, e2, tau = _retrieve(ht, w_q.T.astype(BF16), keys.astype(BF16), tm=tl["retrieve_tm"])
    peer_t = _experts(ht, w_down.astype(BF16), w_up.T.astype(BF16), s1, w1, s2, e2, tau,
                      tm=tl["experts_tm"], rows=tl["experts_rows"])
    return x1, peer_t


def kernel(x, mem, norm_mix_g, norm_mem_g, w_in, gmlp_w_s, gmlp_b_s, gmlp_ln_g, gmlp_ln_b, conv_w,
           w_kv, w_branch, w_out, norm_ffn_g, peer_w_q, peer_keys, peer_w_down, peer_w_up,
           norm_final_g):
    bsz, seq, d = x.shape
    depth = w_in.shape[0]
    x2 = x.reshape(bsz * seq, d)
    t = bsz * seq
    for l in range(depth):
        x1, peer_t = _layer(x2, mem, seq, norm_mix_g[l], norm_mem_g[l], w_in[l], gmlp_w_s[l],
                            gmlp_b_s[l], gmlp_ln_g[l], gmlp_ln_b[l], conv_w[l], w_kv[l],
                            w_branch[l], w_out[l], norm_ffn_g[l], peer_w_q[l], peer_keys[l],
                            peer_w_down[l], peer_w_up[l])
        x2 = _final(x1, peer_t, norm_final_g.reshape(1, d), tm=min(512, t), normalize=l == depth - 1)
    return x2.reshape(bsz, seq, d)
```
